```python
import math, functools
import jax, jax.numpy as jnp
from jax import lax
import numpy as np

D_MODEL = 4096
BATCH = 1
SEQ = 8192
DEPTH = 1
DEC_BATCH = 128
DEC_SEQ = 4
PAST_LEN = 2048
PAGE_SIZE = 128

W_POOL = D_MODEL // 2
POOL_WINDOWS = (2, 4, 8, 16)
N_POOL_GROUPS = len(POOL_WINDOWS)
POOL_GROUP = W_POOL // N_POOL_GROUPS
POOL_BUF = max(POOL_WINDOWS) - 1
HEAD_DIM = 128
N_HEADS = (D_MODEL // 2) // HEAD_DIM
N_KV_HEADS = N_HEADS // 2
GQA = N_HEADS // N_KV_HEADS
W_ATTN = N_HEADS * HEAD_DIM
W_KV = N_KV_HEADS * HEAD_DIM
MOBA_BLOCK = 256
MOBA_TOPK = 3
MOBA_Q_CHUNK = 32
ROT_DIM = HEAD_DIM // 4
ROPE_THETA = 500000.0
EPS = 1e-6
NEG = -1e30

_SIZES = (W_POOL, W_POOL, W_ATTN, W_KV, W_KV, W_ATTN, D_MODEL, D_MODEL)
SPLIT_IDX = tuple(int(s) for s in np.cumsum(_SIZES)[:-1])
D_IN = int(sum(_SIZES))

kernel_name = "hybrid_pool_moba_gated_decode_step"


def _rmsnorm(x, g):
    xf = x.astype(jnp.float32)
    y = xf * lax.rsqrt(jnp.mean(xf * xf, axis=-1, keepdims=True) + EPS)
    return (y * g.astype(jnp.float32)).astype(x.dtype)


def _rope(x, pos):
    inv = ROPE_THETA ** (-jnp.arange(0, ROT_DIM, 2, dtype=jnp.float32) / ROT_DIM)
    ang = pos.astype(jnp.float32)[:, None] * inv[None, :]
    c = jnp.cos(ang)[None, :, None, :]
    s = jnp.sin(ang)[None, :, None, :]
    xf = x.astype(jnp.float32)
    half = ROT_DIM // 2
    x1 = xf[..., :half]
    x2 = xf[..., half:ROT_DIM]
    out = jnp.concatenate([x1 * c - x2 * s, x2 * c + x1 * s, xf[..., ROT_DIM:]], axis=-1)
    return out.astype(x.dtype)


def _pool_mix(ext, n_prefix, w_pool, pool_scale):
    B_, LE, _ = ext.shape
    L = LE - n_prefix
    cs = jnp.cumsum(ext.astype(jnp.float32), axis=1)
    cs = jnp.concatenate([jnp.zeros_like(cs[:, :1]), cs], axis=1)
    i = n_prefix + jnp.arange(L)
    hi = cs[:, i + 1]
    u = ext[:, n_prefix:].astype(jnp.float32)
    outs = []
    for g, w in enumerate(POOL_WINDOWS):
        sl = slice(g * POOL_GROUP, (g + 1) * POOL_GROUP)
        lo = jnp.maximum(i + 1 - w, 0)
        cnt = (i + 1 - lo).astype(jnp.float32)[None, :, None]
        mean = (hi[..., sl] - cs[:, lo][..., sl]) / cnt
        outs.append(mean - u[..., sl])
    d = jnp.stack(outs, axis=2)
    y = jnp.einsum('blgc,gcd->blgd', d, w_pool.astype(jnp.float32)).reshape(B_, L, W_POOL)
    return (y * pool_scale.astype(jnp.float32)).astype(ext.dtype)


def _blocks(k):
    L = k.shape[0]
    nb = max(-(-L // MOBA_BLOCK), MOBA_TOPK)
    kp = jnp.pad(k, ((0, nb * MOBA_BLOCK - L), (0, 0), (0, 0)))
    return kp.reshape(nb, MOBA_BLOCK, N_KV_HEADS, HEAD_DIM).transpose(2, 0, 1, 3)


def _moba_chunk(q, q_pos, kb, vb, kmean):
    qc = q.shape[0]
    nb = kb.shape[1]
    own = q_pos // MOBA_BLOCK
    qf = q.astype(jnp.float32)
    scale = HEAD_DIM ** -0.5
    gate = jnp.einsum('qkgd,knd->qkgn', qf, kmean)
    past_blk = (jnp.arange(nb)[None, :] < own[:, None])[:, None, None, :]
    gate = jnp.where(past_blk, gate, NEG)
    _, idx = lax.top_k(gate, MOBA_TOPK)
    valid = (jnp.arange(MOBA_TOPK)[None, :] < own[:, None])[:, None, None, :, None]
    kv = jnp.arange(N_KV_HEADS)
    k_sel = kb[kv[None, :, None, None], idx].astype(jnp.float32)
    v_sel = vb[kv[None, :, None, None], idx].astype(jnp.float32)
    s_sel = jnp.einsum('qkgd,qkgjrd->qkgjr', qf, k_sel) * scale
    s_sel = jnp.where(valid, s_sel, NEG).reshape(qc, N_KV_HEADS, GQA, MOBA_TOPK * MOBA_BLOCK)
    k_own = kb[kv[None, :], own[:, None]].astype(jnp.float32)
    v_own = vb[kv[None, :], own[:, None]].astype(jnp.float32)
    s_own = jnp.einsum('qkgd,qkrd->qkgr', qf, k_own) * scale
    key_pos = own[:, None] * MOBA_BLOCK + jnp.arange(MOBA_BLOCK)[None, :]
    s_own = jnp.where((key_pos <= q_pos[:, None])[:, None, None, :], s_own, NEG)
    p = jax.nn.softmax(jnp.concatenate([s_sel, s_own], axis=-1), axis=-1)
    n_sel = MOBA_TOPK * MOBA_BLOCK
    p_sel = p[..., :n_sel].reshape(qc, N_KV_HEADS, GQA, MOBA_TOPK, MOBA_BLOCK)
    o = (jnp.einsum('qkgjr,qkgjrd->qkgd', p_sel, v_sel)
         + jnp.einsum('qkgr,qkrd->qkgd', p[..., n_sel:], v_own))
    return o.reshape(qc, W_ATTN).astype(q.dtype)


def _moba_prompt(q, k, v):
    S = q.shape[1]
    nc = S // MOBA_Q_CHUNK
    pos = jnp.arange(S, dtype=jnp.int32).reshape(nc, MOBA_Q_CHUNK)

    def one_seq(args):
        qs, ks_, vs = args
        kb, vb = _blocks(ks_), _blocks(vs)
        kmean = kb.astype(jnp.float32).mean(axis=2)
        qcs = qs.reshape(nc, MOBA_Q_CHUNK, N_KV_HEADS, GQA, HEAD_DIM)
        out = lax.map(lambda c: _moba_chunk(c[0], c[1], kb, vb, kmean), (qcs, pos))
        return out.reshape(S, W_ATTN)

    return lax.map(one_seq, (q, k, v))


def _moba_sample(q, k, v, cache_k, cache_v, page_table):
    T = q.shape[1]
    past = page_table.shape[1] * PAGE_SIZE
    pos = past + jnp.arange(T, dtype=jnp.int32)

    def one_seq(args):
        qs, kn, vn, pt = args
        k_all = jnp.concatenate([cache_k[pt].reshape(past, N_KV_HEADS, HEAD_DIM).astype(kn.dtype), kn], axis=0)
        v_all = jnp.concatenate([cache_v[pt].reshape(past, N_KV_HEADS, HEAD_DIM).astype(vn.dtype), vn], axis=0)
        kb, vb = _blocks(k_all), _blocks(v_all)
        kmean = kb.astype(jnp.float32).mean(axis=2)
        return _moba_chunk(qs.reshape(T, N_KV_HEADS, GQA, HEAD_DIM), pos, kb, vb, kmean)

    return lax.map(one_seq, (q, k, v, page_table))


def _layer(x, pool_buf, pos, attend, norm_g, w_in, w_pool, pool_scale, w_bp, w_ba, w_o):
    B_, L, _ = x.shape
    h = _rmsnorm(x, norm_g)
    proj = jnp.einsum('bld,de->ble', h, w_in)
    u, z_p, q, k, v, z_a, g_p, g_a = jnp.split(proj, SPLIT_IDX, axis=-1)
    if pool_buf is None:
        ext, n_prefix = u, 0
    else:
        ext = jnp.concatenate([pool_buf.astype(u.dtype), u], axis=1)
        n_prefix = pool_buf.shape[1]
    a_pool = _pool_mix(ext, n_prefix, w_pool, pool_scale) * jax.nn.silu(z_p)
    new_buf = ext[:, -POOL_BUF:]
    q = _rope(q.reshape(B_, L, N_HEADS, HEAD_DIM), pos)
    k = _rope(k.reshape(B_, L, N_KV_HEADS, HEAD_DIM), pos)
    v = v.reshape(B_, L, N_KV_HEADS, HEAD_DIM)
    a_attn = attend(q, k, v) * jax.nn.silu(z_a)
    merged = (jax.nn.sigmoid(g_p) * jnp.einsum('blc,cd->bld', a_pool, w_bp)
              + jax.nn.sigmoid(g_a) * jnp.einsum('blc,cd->bld', a_attn, w_ba))
    y = x + jnp.einsum('bld,de->ble', merged, w_o)
    return y, k, v, new_buf


def setup_inputs(seed: int = 0) -> dict:
    key = jax.random.key(seed)
    ks = jax.random.split(key, 16)
    n_pages = PAST_LEN // PAGE_SIZE
    n_used = DEC_BATCH * n_pages
    n_phys = n_used + n_used // 4
    perm = jax.random.permutation(ks[0], n_phys)
    page_table = perm[:n_used].reshape(DEC_BATCH, n_pages).astype(jnp.int32)
    f32 = jnp.float32
    nrm = lambda k, shape, s: jax.random.normal(k, shape, f32) * s
    return {
        "x_prompt": nrm(ks[1], (BATCH, SEQ, D_MODEL), 1.0),
        "x_sample": nrm(ks[2], (DEC_BATCH, DEC_SEQ, D_MODEL), 1.0),
        "cache_k": nrm(ks[3], (DEPTH, n_phys, PAGE_SIZE, N_KV_HEADS, HEAD_DIM), 1.0),
        "cache_v": nrm(ks[4], (DEPTH, n_phys, PAGE_SIZE, N_KV_HEADS, HEAD_DIM), 1.0),
        "state_pool": nrm(ks[5], (DEPTH, DEC_BATCH, POOL_BUF, W_POOL), 1.0),
        "page_table": page_table,
        "norm_in": 1.0 + nrm(ks[6], (DEPTH, D_MODEL), 0.1),
        "w_in": nrm(ks[7], (DEPTH, D_MODEL, D_IN), D_MODEL ** -0.5),
        "w_pool": nrm(ks[8], (DEPTH, N_POOL_GROUPS, POOL_GROUP, POOL_GROUP), POOL_GROUP ** -0.5),
        "pool_scale": 1.0 + nrm(ks[9], (DEPTH, W_POOL), 0.1),
        "w_branch_pool": nrm(ks[10], (DEPTH, W_POOL, D_MODEL), W_POOL ** -0.5),
        "w_branch_attn": nrm(ks[11], (DEPTH, W_ATTN, D_MODEL), W_ATTN ** -0.5),
        "w_out": nrm(ks[12], (DEPTH, D_MODEL, D_MODEL), D_MODEL ** -0.5),
        "norm_final": 1.0 + nrm(ks[13], (D_MODEL,), 0.1),
    }


def reference(x_prompt, x_sample, cache_k, cache_v, state_pool, page_table,
              norm_in, w_in, w_pool, pool_scale, w_branch_pool, w_branch_attn, w_out, norm_final):
    pos_p = jnp.arange(x_prompt.shape[1], dtype=jnp.int32)
    past = page_table.shape[1] * PAGE_SIZE
    pos_s = past + jnp.arange(x_sample.shape[1], dtype=jnp.int32)
    hp, hs = x_prompt, x_sample
    kp_l, vp_l, bp_l, ks_l, vs_l, bs_l = [], [], [], [], [], []
    for l in range(DEPTH):
        params = (norm_in[l], w_in[l], w_pool[l], pool_scale[l], w_branch_pool[l], w_branch_attn[l], w_out[l])
        hp, kp, vp, bp = _layer(hp, None, pos_p, _moba_prompt, *params)
        attend_s = functools.partial(_moba_sample, cache_k=cache_k[l], cache_v=cache_v[l], page_table=page_table)
        hs, ks_, vs, bs = _layer(hs, state_pool[l], pos_s, attend_s, *params)
        kp_l.append(kp); vp_l.append(vp); bp_l.append(bp)
        ks_l.append(ks_); vs_l.append(vs); bs_l.append(bs)
    y_prompt = _rmsnorm(hp, norm_final)
    y_sample = _rmsnorm(hs, norm_final)
    return (y_prompt, y_sample, jnp.stack(kp_l), jnp.stack(vp_l), jnp.stack(bp_l),
            jnp.stack(ks_l), jnp.stack(vs_l), jnp.stack(bs_l))
```

```python
import functools

import jax
import jax.numpy as jnp
from jax import lax
from jax.experimental import pallas as pl
from jax.experimental.pallas import tpu as pltpu

D_MODEL = 4096
W_POOL = D_MODEL // 2
POOL_WINDOWS = (2, 4, 8, 16)
N_POOL_GROUPS = len(POOL_WINDOWS)
POOL_GROUP = W_POOL // N_POOL_GROUPS
POOL_BUF = max(POOL_WINDOWS) - 1
HEAD_DIM = 128
N_HEADS = (D_MODEL // 2) // HEAD_DIM
N_KV_HEADS = N_HEADS // 2
GQA = N_HEADS // N_KV_HEADS
W_ATTN = N_HEADS * HEAD_DIM
W_KV = N_KV_HEADS * HEAD_DIM
MOBA_BLOCK = 256
MOBA_TOPK = 3
ROT_DIM = HEAD_DIM // 4
ROPE_THETA = 500000.0
EPS = 1e-6
NEG = -1e30
PAGE_SIZE = 128
PAGES_PER_BLOCK = MOBA_BLOCK // PAGE_SIZE

OFF_U = 0
OFF_ZP = OFF_U + W_POOL
OFF_Q = OFF_ZP + W_POOL
OFF_K = OFF_Q + W_ATTN
OFF_V = OFF_K + W_KV
OFF_ZA = OFF_V + W_KV
OFF_GP = OFF_ZA + W_ATTN
OFF_GA = OFF_GP + D_MODEL
D_IN = OFF_GA + D_MODEL

SM_SCALE = HEAD_DIM ** -0.5
HALO = 16
ROW_TILE = 256
VMEM_LIMIT = 56 * 1024 * 1024


def _params(*sem):
    return pltpu.CompilerParams(dimension_semantics=sem, vmem_limit_bytes=VMEM_LIMIT)


def _sigmoid(x):
    return 1.0 / (1.0 + jnp.exp(-x))


def _silu(x):
    return x * _sigmoid(x)


def _rmsnorm_kernel(x_ref, g_ref, o_ref):
    x = x_ref[...]
    ms = jnp.mean(x * x, axis=-1, keepdims=True)
    o_ref[...] = (x * lax.rsqrt(ms + EPS) * g_ref[...]).astype(o_ref.dtype)


def _rmsnorm(x, g, out_dtype):
    t, d = x.shape
    return pl.pallas_call(
        _rmsnorm_kernel,
        out_shape=jax.ShapeDtypeStruct((t, d), out_dtype),
        grid=(t // ROW_TILE,),
        in_specs=[pl.BlockSpec((ROW_TILE, d), lambda i: (i, 0)),
                  pl.BlockSpec((1, d), lambda i: (0, 0))],
        out_specs=pl.BlockSpec((ROW_TILE, d), lambda i: (i, 0)),
        compiler_params=_params("parallel"),
        name="rmsnorm",
    )(x, g.reshape(1, d))


def _matmul_kernel(a_ref, b_ref, o_ref):
    o_ref[...] = jnp.dot(a_ref[...], b_ref[...], preferred_element_type=jnp.float32)


def _matmul(a, b, tm, tn):
    t, k = a.shape
    n = b.shape[1]
    return pl.pallas_call(
        _matmul_kernel,
        out_shape=jax.ShapeDtypeStruct((t, n), jnp.float32),
        grid=(t // tm, n // tn),
        in_specs=[pl.BlockSpec((tm, k), lambda i, j: (i, 0)),
                  pl.BlockSpec((k, tn), lambda i, j: (0, j))],
        out_specs=pl.BlockSpec((tm, tn), lambda i, j: (i, j)),
        compiler_params=_params("parallel", "arbitrary"),
        name="in_proj",
    )(a, b)


def _pool_prompt_kernel(u_ref, halo_ref, z_ref, w_ref, sc_ref, o_ref):
    not_first = jnp.minimum(pl.program_id(0), 1) * ROW_TILE
    r = lax.broadcasted_iota(jnp.int32, (ROW_TILE, ROW_TILE), 0)
    c = lax.broadcasted_iota(jnp.int32, (ROW_TILE, ROW_TILE), 1)
    rh = lax.broadcasted_iota(jnp.int32, (ROW_TILE, HALO), 0)
    ch = lax.broadcasted_iota(jnp.int32, (ROW_TILE, HALO), 1)
    rcol = lax.broadcasted_iota(jnp.int32, (ROW_TILE, 1), 0)
    for g, w in enumerate(POOL_WINDOWS):
        sl = slice(g * POOL_GROUP, (g + 1) * POOL_GROUP)
        band = ((c <= r) & (c > r - w)).astype(jnp.float32).astype(jnp.bfloat16)
        band_halo = (ch + not_first > rh - w + HALO + ROW_TILE).astype(jnp.float32).astype(jnp.bfloat16)
        cnt = jnp.minimum(rcol + 1 + not_first, w).astype(jnp.float32)
        u = u_ref[:, sl]
        wsum = (jnp.dot(band, u.astype(jnp.bfloat16), preferred_element_type=jnp.float32)
                + jnp.dot(band_halo, halo_ref[:, sl].astype(jnp.bfloat16), preferred_element_type=jnp.float32))
        d = wsum / cnt - u
        y = jnp.dot(d.astype(jnp.bfloat16), w_ref[g], preferred_element_type=jnp.float32)
        o_ref[:, sl] = (y * sc_ref[:, sl] * _silu(z_ref[:, sl])).astype(o_ref.dtype)


def _pool_prompt(proj, w_pool, pool_scale):
    t = proj.shape[0]
    halo_blocks = ROW_TILE // HALO
    return pl.pallas_call(
        _pool_prompt_kernel,
        out_shape=jax.ShapeDtypeStruct((t, W_POOL), jnp.bfloat16),
        grid=(t // ROW_TILE,),
        in_specs=[pl.BlockSpec((ROW_TILE, W_POOL), lambda i: (i, OFF_U // W_POOL)),
                  pl.BlockSpec((HALO, W_POOL), lambda i: (jnp.maximum(i * halo_blocks - 1, 0), OFF_U // W_POOL)),
                  pl.BlockSpec((ROW_TILE, W_POOL), lambda i: (i, OFF_ZP // W_POOL)),
                  pl.BlockSpec((N_POOL_GROUPS, POOL_GROUP, POOL_GROUP), lambda i: (0, 0, 0)),
                  pl.BlockSpec((1, W_POOL), lambda i: (0, 0))],
        out_specs=pl.BlockSpec((ROW_TILE, W_POOL), lambda i: (i, 0)),
        compiler_params=_params("parallel"),
        name="pool_prompt",
    )(proj, proj, proj, w_pool, pool_scale)


def _pool_sample_kernel(u_ref, st_ref, z_ref, w_ref, sc_ref, o_ref, *, dec_seq):
    n_rows = u_ref.shape[0]
    n_state = st_ref.shape[0]
    w = jnp.left_shift(2, pl.program_id(0))
    seq_shift = dec_seq.bit_length() - 1
    r = lax.broadcasted_iota(jnp.int32, (n_rows, n_rows), 0)
    c = lax.broadcasted_iota(jnp.int32, (n_rows, n_rows), 1)
    same_seq = (r >> seq_shift) == (c >> seq_shift)
    band = (same_seq & (c <= r) & (r - c < w)).astype(jnp.float32).astype(jnp.bfloat16)
    rs = lax.broadcasted_iota(jnp.int32, (n_rows, n_state), 0)
    cs = lax.broadcasted_iota(jnp.int32, (n_rows, n_state), 1)
    base = (rs >> seq_shift) * POOL_BUF
    tok = rs & (dec_seq - 1)
    band_state = ((cs >= base + POOL_BUF + 1 + tok - w) & (cs < base + POOL_BUF)).astype(jnp.float32).astype(jnp.bfloat16)
    u = u_ref[...]
    wsum = (jnp.dot(band, u.astype(jnp.bfloat16), preferred_element_type=jnp.float32)
            + jnp.dot(band_state, st_ref[...].astype(jnp.bfloat16), preferred_element_type=jnp.float32))
    cnt = (jnp.zeros((n_rows, 1), jnp.int32) + w).astype(jnp.float32)
    d = wsum / cnt - u
    y = jnp.dot(d.astype(jnp.bfloat16), w_ref[0], preferred_element_type=jnp.float32)
    o_ref[...] = (y * sc_ref[...] * _silu(z_ref[...])).astype(o_ref.dtype)


def _pool_sample(proj, state2d, w_pool, pool_scale, dec_seq):
    t = proj.shape[0]
    n_state = state2d.shape[0]
    return pl.pallas_call(
        functools.partial(_pool_sample_kernel, dec_seq=dec_seq),
        out_shape=jax.ShapeDtypeStruct((t, W_POOL), jnp.bfloat16),
        grid=(N_POOL_GROUPS,),
        in_specs=[pl.BlockSpec((t, POOL_GROUP), lambda g: (0, OFF_U // POOL_GROUP + g)),
                  pl.BlockSpec((n_state, POOL_GROUP), lambda g: (0, g)),
                  pl.BlockSpec((t, POOL_GROUP), lambda g: (0, OFF_ZP // POOL_GROUP + g)),
                  pl.BlockSpec((1, POOL_GROUP, POOL_GROUP), lambda g: (g, 0, 0)),
                  pl.BlockSpec((1, POOL_GROUP), lambda g: (0, g))],
        out_specs=pl.BlockSpec((t, POOL_GROUP), lambda g: (0, g)),
        compiler_params=_params("parallel"),
        name="pool_sample",
    )(proj, state2d, proj, w_pool, pool_scale)


def _rope_kernel(q_ref, k_ref, v_ref, c_ref, sa_ref, sb_ref,
                 qo_ref, ko_ref, kb_ref, vo_ref, vb_ref, *maybe_kmean_ref):
    cos, sa, sb = c_ref[...], sa_ref[...], sb_ref[...]

    def rot(x):
        return x * cos + pltpu.roll(x, HEAD_DIM - ROT_DIM // 2, 1) * sa + pltpu.roll(x, ROT_DIM // 2, 1) * sb

    for hd in range(N_HEADS):
        sl = slice(hd * HEAD_DIM, (hd + 1) * HEAD_DIM)
        qo_ref[:, sl] = (rot(q_ref[:, sl]) * SM_SCALE).astype(qo_ref.dtype)
    for hd in range(N_KV_HEADS):
        sl = slice(hd * HEAD_DIM, (hd + 1) * HEAD_DIM)
        kr = rot(k_ref[:, sl])
        ko_ref[:, sl] = kr
        kb_ref[:, sl] = kr.astype(kb_ref.dtype)
        if maybe_kmean_ref:
            maybe_kmean_ref[0][0, :, sl] = jnp.mean(kr, axis=0, keepdims=True)
    v = v_ref[...]
    vo_ref[...] = v
    vb_ref[...] = v.astype(vb_ref.dtype)


def _rope(proj, cos, sa, sb, with_kmean):
    t = proj.shape[0]
    nt = t // ROW_TILE
    row = lambda i: (i, 0)
    out_shape = [jax.ShapeDtypeStruct((t, W_ATTN), jnp.bfloat16),
                 jax.ShapeDtypeStruct((t, W_KV), jnp.float32),
                 jax.ShapeDtypeStruct((t, W_KV), jnp.bfloat16),
                 jax.ShapeDtypeStruct((t, W_KV), jnp.float32),
                 jax.ShapeDtypeStruct((t, W_KV), jnp.bfloat16)]
    out_specs = [pl.BlockSpec((ROW_TILE, W_ATTN), row),
                 pl.BlockSpec((ROW_TILE, W_KV), row),
                 pl.BlockSpec((ROW_TILE, W_KV), row),
                 pl.BlockSpec((ROW_TILE, W_KV), row),
                 pl.BlockSpec((ROW_TILE, W_KV), row)]
    if with_kmean:
        out_shape.append(jax.ShapeDtypeStruct((nt, 1, W_KV), jnp.float32))
        out_specs.append(pl.BlockSpec((1, 1, W_KV), lambda i: (i, 0, 0)))
    return pl.pallas_call(
        _rope_kernel,
        out_shape=out_shape,
        grid=(nt,),
        in_specs=[pl.BlockSpec((ROW_TILE, W_ATTN), lambda i: (i, OFF_Q // W_ATTN)),
                  pl.BlockSpec((ROW_TILE, W_KV), lambda i: (i, OFF_K // W_KV)),
                  pl.BlockSpec((ROW_TILE, W_KV), lambda i: (i, OFF_V // W_KV)),
                  pl.BlockSpec((ROW_TILE, HEAD_DIM), row),
                  pl.BlockSpec((ROW_TILE, HEAD_DIM), row),
                  pl.BlockSpec((ROW_TILE, HEAD_DIM), row)],
        out_specs=out_specs,
        compiler_params=_params("parallel"),
        name="rope",
    )(proj, proj, proj, cos, sa, sb)


def _rope_tables(pos):
    inv = ROPE_THETA ** (-jnp.arange(0, ROT_DIM, 2, dtype=jnp.float32) / ROT_DIM)
    ang = pos.astype(jnp.float32)[:, None] * inv[None, :]
    c, s = jnp.cos(ang), jnp.sin(ang)
    half = ROT_DIM // 2
    n = pos.shape[0]
    pad = HEAD_DIM - ROT_DIM
    cos = jnp.concatenate([c, c, jnp.ones((n, pad), jnp.float32)], axis=1)
    sa = jnp.concatenate([-s, jnp.zeros((n, half + pad), jnp.float32)], axis=1)
    sb = jnp.concatenate([jnp.zeros((n, half), jnp.float32), s, jnp.zeros((n, pad), jnp.float32)], axis=1)
    return cos, sa, sb


def _select_topk(gate, n_past):
    ncol = gate.shape[1]
    col = lax.broadcasted_iota(jnp.int32, gate.shape, 1)
    colf = col.astype(jnp.float32)
    past = col < n_past
    cur = jnp.where(past, gate, NEG)
    sel = jnp.zeros(gate.shape, jnp.float32)
    for _ in range(MOBA_TOPK):
        top = jnp.max(cur, axis=1, keepdims=True)
        idx = jnp.min(jnp.where(cur == top, colf, float(ncol)), axis=1, keepdims=True)
        hit = colf == idx
        sel = jnp.where(hit & past, 1.0, sel)
        cur = jnp.where(hit, -jnp.inf, cur)
    return sel


def _moba_prompt_kernel(q_ref, k_ref, v_ref, km_ref, z_ref, o_ref):
    i = pl.program_id(1)
    nb = km_ref.shape[0]
    qb = q_ref[...]
    q2 = jnp.concatenate([qb[:, g * HEAD_DIM:(g + 1) * HEAD_DIM] for g in range(GQA)], axis=0)
    rows = q2.shape[0]
    contract_last = (((1,), (1,)), ((), ()))
    gate = lax.dot_general(q2, km_ref[...].astype(jnp.bfloat16), contract_last,
                           preferred_element_type=jnp.float32)
    sel = _select_topk(gate, i).astype(jnp.bfloat16)

    def update(carry, s, vj):
        m, l, acc = carry
        m_new = jnp.maximum(m, jnp.max(s, axis=1, keepdims=True))
        alpha = jnp.exp(m - m_new)
        p = jnp.exp(s - m_new)
        l = alpha * l + jnp.sum(p, axis=1, keepdims=True)
        acc = alpha * acc + jnp.dot(p.astype(jnp.bfloat16), vj, preferred_element_type=jnp.float32)
        return m_new, l, acc

    def past_block(j, carry):
        start = pl.multiple_of(j * MOBA_BLOCK, MOBA_BLOCK)
        kj = k_ref[pl.ds(start, MOBA_BLOCK), :]
        vj = v_ref[pl.ds(start, MOBA_BLOCK), :]
        s = lax.dot_general(q2, kj, contract_last, preferred_element_type=jnp.float32)
        expand = (lax.broadcasted_iota(jnp.int32, (nb, MOBA_BLOCK), 0) == j).astype(jnp.float32).astype(jnp.bfloat16)
        chosen = jnp.dot(sel, expand, preferred_element_type=jnp.float32)
        return update(carry, jnp.where(chosen > 0.5, s, NEG), vj)

    init = (jnp.full((rows, 1), NEG, jnp.float32), jnp.zeros((rows, 1), jnp.float32),
            jnp.zeros((rows, HEAD_DIM), jnp.float32))
    carry = lax.fori_loop(0, i, past_block, init)

    start = pl.multiple_of(i * MOBA_BLOCK, MOBA_BLOCK)
    kd = k_ref[pl.ds(start, MOBA_BLOCK), :]
    vd = v_ref[pl.ds(start, MOBA_BLOCK), :]
    s = lax.dot_general(q2, kd, contract_last, preferred_element_type=jnp.float32)
    qpos = lax.broadcasted_iota(jnp.int32, s.shape, 0) & (MOBA_BLOCK - 1)
    kpos = lax.broadcasted_iota(jnp.int32, s.shape, 1)
    _, l, acc = update(carry, jnp.where(kpos <= qpos, s, NEG), vd)
    out = acc / l
    out = jnp.concatenate([out[g * MOBA_BLOCK:(g + 1) * MOBA_BLOCK] for g in range(GQA)], axis=1)
    o_ref[...] = (out * _silu(z_ref[...])).astype(o_ref.dtype)


def _moba_prompt(q, k, v, kmean, proj):
    t = q.shape[0]
    nb = t // MOBA_BLOCK
    gw = GQA * HEAD_DIM
    return pl.pallas_call(
        _moba_prompt_kernel,
        out_shape=jax.ShapeDtypeStruct((t, W_ATTN), jnp.bfloat16),
        grid=(N_KV_HEADS, nb),
        in_specs=[pl.BlockSpec((MOBA_BLOCK, gw), lambda h, i: (i, h)),
                  pl.BlockSpec((t, HEAD_DIM), lambda h, i: (0, h)),
                  pl.BlockSpec((t, HEAD_DIM), lambda h, i: (0, h)),
                  pl.BlockSpec((nb, HEAD_DIM), lambda h, i: (0, h)),
                  pl.BlockSpec((MOBA_BLOCK, gw), lambda h, i: (i, OFF_ZA // gw + h))],
        out_specs=pl.BlockSpec((MOBA_BLOCK, gw), lambda h, i: (i, h)),
        compiler_params=_params("parallel", "arbitrary"),
        name="moba_prompt",
    )(q, k, v, kmean, proj)


def _moba_sample_kernel(pt_ref, q_ref, k0_ref, k1_ref, v0_ref, v1_ref, kn_ref, vn_ref, z_ref, o_ref,
                        g_s, m_s, l_s, o_s, *, dec_seq):
    j = pl.program_id(1)
    n_blocks = pl.num_programs(1)
    contract_last = (((1,), (1,)), ((), ()))
    rows = GQA * dec_seq
    full = (rows, HEAD_DIM)

    for h in range(N_KV_HEADS):
        qh = q_ref[0, h]
        kp = [r[0, pl.ds(h, PAGE_SIZE, stride=N_KV_HEADS), :] for r in (k0_ref, k1_ref)]
        vp = [r[0, pl.ds(h, PAGE_SIZE, stride=N_KV_HEADS), :] for r in (v0_ref, v1_ref)]
        kmean = sum(jnp.sum(x, axis=0, keepdims=True) for x in kp) * (1.0 / MOBA_BLOCK)
        gate = jnp.sum(qh.astype(jnp.float32) * kmean, axis=1, keepdims=True)
        s = [lax.dot_general(qh, x.astype(jnp.bfloat16), contract_last, preferred_element_type=jnp.float32)
             for x in kp]
        m = functools.reduce(jnp.maximum, [jnp.max(x, axis=1, keepdims=True) for x in s])
        p = [jnp.exp(x - m) for x in s]
        l = sum(jnp.sum(x, axis=1, keepdims=True) for x in p)
        o = sum(jnp.dot(x.astype(jnp.bfloat16), y.astype(jnp.bfloat16), preferred_element_type=jnp.float32)
                for x, y in zip(p, vp))
        g_s[j, h] = jnp.broadcast_to(gate, full)
        m_s[j, h] = jnp.broadcast_to(m, full)
        l_s[j, h] = jnp.broadcast_to(l, full)
        o_s[j, h] = o

    @pl.when(j == n_blocks - 1)
    def _():
        nb = g_s.shape[0]
        tok = lax.broadcasted_iota(jnp.int32, full, 0) >> (GQA.bit_length() - 1)
        for h in range(N_KV_HEADS):
            gates = [g_s[b, h] for b in range(nb)]
            sel = [jnp.zeros(full, jnp.bool_) for _ in range(nb)]
            for _ in range(min(MOBA_TOPK, nb)):
                top = functools.reduce(jnp.maximum, gates)
                idx = functools.reduce(jnp.minimum,
                                       [jnp.where(gates[b] == top, float(b), float(nb)) for b in range(nb)])
                for b in range(nb):
                    hit = idx == float(b)
                    sel[b] = sel[b] | hit
                    gates[b] = jnp.where(hit, -jnp.inf, gates[b])
            qf = q_ref[0, h].astype(jnp.float32)
            sl = slice(h * HEAD_DIM, (h + 1) * HEAD_DIM)
            s_new = [jnp.broadcast_to(jnp.sum(qf * kn_ref[0, r:r + 1, sl], axis=1, keepdims=True), full)
                     for r in range(dec_seq)]
            ok_new = [tok >= r for r in range(dec_seq)]
            ms = [m_s[b, h] for b in range(nb)]
            top = functools.reduce(jnp.maximum,
                                   [jnp.where(sel[b], ms[b], NEG) for b in range(nb)]
                                   + [jnp.where(ok_new[r], s_new[r], NEG) for r in range(dec_seq)])
            num = jnp.zeros(full, jnp.float32)
            den = jnp.zeros(full, jnp.float32)
            for b in range(nb):
                wgt = jnp.where(sel[b], jnp.exp(ms[b] - top), 0.0)
                num = num + wgt * o_s[b, h]
                den = den + wgt * l_s[b, h]
            for r in range(dec_seq):
                wgt = jnp.where(ok_new[r], jnp.exp(s_new[r] - top), 0.0)
                num = num + wgt * vn_ref[0, r:r + 1, sl]
                den = den + wgt
            o_ref[0, h] = (num / den * _silu(z_ref[0, h])).astype(o_ref.dtype)


def _moba_sample(page_table, q4, cache_k, cache_v, k_new, v_new, z4, dec_seq):
    nseq, n_pages = page_table.shape
    n_blocks = n_pages // PAGES_PER_BLOCK
    rows = GQA * dec_seq
    page_rows = PAGE_SIZE * N_KV_HEADS
    seq4 = pl.BlockSpec((1, N_KV_HEADS, rows, HEAD_DIM), lambda b, j, pt: (b, 0, 0, 0))
    page = lambda off: pl.BlockSpec((1, page_rows, HEAD_DIM),
                                    lambda b, j, pt: (pt[b, PAGES_PER_BLOCK * j + off], 0, 0))
    new = pl.BlockSpec((1, dec_seq, W_KV), lambda b, j, pt: (b, 0, 0))
    stat = pltpu.VMEM((n_blocks, N_KV_HEADS, rows, HEAD_DIM), jnp.float32)
    return pl.pallas_call(
        functools.partial(_moba_sample_kernel, dec_seq=dec_seq),
        out_shape=jax.ShapeDtypeStruct((nseq, N_KV_HEADS, rows, HEAD_DIM), jnp.bfloat16),
        grid_spec=pltpu.PrefetchScalarGridSpec(
            num_scalar_prefetch=1,
            grid=(nseq, n_blocks),
            in_specs=[seq4, page(0), page(1), page(0), page(1), new, new, seq4],
            out_specs=seq4,
            scratch_shapes=[stat, stat, stat, stat]),
        compiler_params=_params("parallel", "arbitrary"),
        name="moba_sample",
    )(page_table, q4, cache_k, cache_k, cache_v, cache_v, k_new, v_new, z4)


def _merge_kernel(ap_ref, aa_ref, wp_ref, wa_ref, gp_ref, ga_ref, o_ref):
    bp = jnp.dot(ap_ref[...], wp_ref[...], preferred_element_type=jnp.float32)
    ba = jnp.dot(aa_ref[...], wa_ref[...], preferred_element_type=jnp.float32)
    o_ref[...] = (_sigmoid(gp_ref[...]) * bp + _sigmoid(ga_ref[...]) * ba).astype(o_ref.dtype)


def _merge(a_pool, a_attn, w_bp, w_ba, proj, tm, tn):
    t = a_pool.shape[0]
    return pl.pallas_call(
        _merge_kernel,
        out_shape=jax.ShapeDtypeStruct((t, D_MODEL), jnp.bfloat16),
        grid=(t // tm, D_MODEL // tn),
        in_specs=[pl.BlockSpec((tm, W_POOL), lambda i, j: (i, 0)),
                  pl.BlockSpec((tm, W_ATTN), lambda i, j: (i, 0)),
                  pl.BlockSpec((W_POOL, tn), lambda i, j: (0, j)),
                  pl.BlockSpec((W_ATTN, tn), lambda i, j: (0, j)),
                  pl.BlockSpec((tm, tn), lambda i, j: (i, OFF_GP // tn + j)),
                  pl.BlockSpec((tm, tn), lambda i, j: (i, OFF_GA // tn + j))],
        out_specs=pl.BlockSpec((tm, tn), lambda i, j: (i, j)),
        compiler_params=_params("parallel", "arbitrary"),
        name="merge",
    )(a_pool, a_attn, w_bp, w_ba, proj, proj)


def _out_proj_kernel(m_ref, w_ref, x_ref, o_ref):
    o_ref[...] = x_ref[...] + jnp.dot(m_ref[...], w_ref[...], preferred_element_type=jnp.float32)


def _out_proj(merged, w_o, x, tm, tn):
    t = merged.shape[0]
    return pl.pallas_call(
        _out_proj_kernel,
        out_shape=jax.ShapeDtypeStruct((t, D_MODEL), jnp.float32),
        grid=(t // tm, D_MODEL // tn),
        in_specs=[pl.BlockSpec((tm, D_MODEL), lambda i, j: (i, 0)),
                  pl.BlockSpec((D_MODEL, tn), lambda i, j: (0, j)),
                  pl.BlockSpec((tm, tn), lambda i, j: (i, j))],
        out_specs=pl.BlockSpec((tm, tn), lambda i, j: (i, j)),
        compiler_params=_params("parallel", "arbitrary"),
        name="out_proj",
    )(merged, w_o, x)


def _to_kv_rows(x, nseq, dec_seq):
    x = x.reshape(nseq, dec_seq, N_KV_HEADS, GQA, HEAD_DIM)
    return x.transpose(0, 2, 1, 3, 4).reshape(nseq, N_KV_HEADS, dec_seq * GQA, HEAD_DIM)


def _from_kv_rows(x, nseq, dec_seq):
    x = x.reshape(nseq, N_KV_HEADS, dec_seq, GQA, HEAD_DIM)
    return x.transpose(0, 2, 1, 3, 4).reshape(nseq * dec_seq, W_ATTN)


def kernel(x_prompt, x_sample, cache_k, cache_v, state_pool, page_table, norm_in, w_in, w_pool, pool_scale,
           w_branch_pool, w_branch_attn, w_out, norm_final):
    batch, seq, _ = x_prompt.shape
    nseq, dec_seq, _ = x_sample.shape
    depth, n_phys = cache_k.shape[:2]
    assert batch == 1 and depth == 1
    past = page_table.shape[1] * PAGE_SIZE
    assert past % MOBA_BLOCK == 0 and dec_seq <= MOBA_BLOCK
    assert dec_seq & (dec_seq - 1) == 0 and GQA & (GQA - 1) == 0
    bf16 = jnp.bfloat16

    w_in_b = w_in[0].astype(bf16)
    w_pool_b = w_pool[0].astype(bf16)
    w_bp_b = w_branch_pool[0].astype(bf16)
    w_ba_b = w_branch_attn[0].astype(bf16)
    w_o_b = w_out[0].astype(bf16)
    scale2d = pool_scale[0].reshape(1, W_POOL)

    def trunk(x2d, pos, tm, pool_fn, attend_fn):
        h = _rmsnorm(x2d, norm_in[0], bf16)
        proj = _matmul(h, w_in_b, tm, 1024)
        a_pool = pool_fn(proj)
        q, k, kb, v, vb, *km = _rope(proj, *_rope_tables(pos), with_kmean=attend_fn is _attend_prompt)
        a_attn = attend_fn(proj, q, k, kb, v, vb, *km)
        merged = _merge(a_pool, a_attn, w_bp_b, w_ba_b, proj, 512, 1024)
        y = _out_proj(merged, w_o_b, x2d, 512, 1024)
        return _rmsnorm(y, norm_final, jnp.float32), k, v, proj

    def _attend_prompt(proj, q, k, kb, v, vb, km):
        return _moba_prompt(q, kb, vb, km.reshape(km.shape[0], W_KV), proj)

    def _attend_sample(proj, q, k, kb, v, vb):
        q4 = _to_kv_rows(q, nseq, dec_seq)
        z4 = _to_kv_rows(proj[:, OFF_ZA:OFF_ZA + W_ATTN], nseq, dec_seq)
        ck = cache_k[0].reshape(n_phys, PAGE_SIZE * N_KV_HEADS, HEAD_DIM)
        cv = cache_v[0].reshape(n_phys, PAGE_SIZE * N_KV_HEADS, HEAD_DIM)
        a4 = _moba_sample(page_table, q4, ck, cv, k.reshape(nseq, dec_seq, W_KV),
                          v.reshape(nseq, dec_seq, W_KV), z4, dec_seq)
        return _from_kv_rows(a4, nseq, dec_seq)

    pos_p = jnp.arange(seq, dtype=jnp.int32)
    pos_s = jnp.tile(past + jnp.arange(dec_seq, dtype=jnp.int32), nseq)
    state2d = state_pool[0].reshape(nseq * POOL_BUF, W_POOL)

    y_p, k_p, v_p, proj_p = trunk(
        x_prompt.reshape(seq, D_MODEL), pos_p, 1024,
        lambda proj: _pool_prompt(proj, w_pool_b, scale2d), _attend_prompt)
    y_s, k_s, v_s, proj_s = trunk(
        x_sample.reshape(nseq * dec_seq, D_MODEL), pos_s, nseq * dec_seq,
        lambda proj: _pool_sample(proj, state2d, w_pool_b, scale2d, dec_seq), _attend_sample)

    u_s = proj_s[:, OFF_U:OFF_U + W_POOL].reshape(nseq, dec_seq, W_POOL)
    pool_s = jnp.concatenate([state_pool[0], u_s], axis=1)[:, -POOL_BUF:]
    pool_p = proj_p[seq - POOL_BUF:, OFF_U:OFF_U + W_POOL]
    return (y_p.reshape(1, seq, D_MODEL),
            y_s.reshape(nseq, dec_seq, D_MODEL),
            k_p.reshape(1, 1, seq, N_KV_HEADS, HEAD_DIM),
            v_p.reshape(1, 1, seq, N_KV_HEADS, HEAD_DIM),
            pool_p.reshape(1, 1, POOL_BUF, W_POOL),
            k_s.reshape(1, nseq, dec_seq, N_KV_HEADS, HEAD_DIM),
            v_s.reshape(1, nseq, dec_seq, N_KV_HEADS, HEAD_DIM),
            pool_s.reshape(1, nseq, POOL_BUF, W_POOL))
```

```python
import functools

import jax
import jax.numpy as jnp
from jax import lax
from jax.experimental import pallas as pl
from jax.experimental.pallas import tpu as pltpu

D_MODEL = 4096
W_POOL = D_MODEL // 2
POOL_WINDOWS = (2, 4, 8, 16)
N_POOL_GROUPS = len(POOL_WINDOWS)
POOL_GROUP = W_POOL // N_POOL_GROUPS
POOL_BUF = max(POOL_WINDOWS) - 1
HEAD_DIM = 128
N_HEADS = (D_MODEL // 2) // HEAD_DIM
N_KV_HEADS = N_HEADS // 2
GQA = N_HEADS // N_KV_HEADS
W_ATTN = N_HEADS * HEAD_DIM
W_KV = N_KV_HEADS * HEAD_DIM
MOBA_BLOCK = 256
MOBA_TOPK = 3
ROT_DIM = HEAD_DIM // 4
ROPE_THETA = 500000.0
EPS = 1e-6
NEG = -1e30
PAGE_SIZE = 128
PAGES_PER_BLOCK = MOBA_BLOCK // PAGE_SIZE

OFF_U = 0
OFF_ZP = OFF_U + W_POOL
OFF_Q = OFF_ZP + W_POOL
OFF_K = OFF_Q + W_ATTN
OFF_V = OFF_K + W_KV
OFF_ZA = OFF_V + W_KV
OFF_GP = OFF_ZA + W_ATTN
OFF_GA = OFF_GP + D_MODEL
D_IN = OFF_GA + D_MODEL

LOG2E = 1.4426950408889634
Q_SCALE = HEAD_DIM ** -0.5 * LOG2E
HALO = 16
ROW_TILE = 256
SAMPLE_BLOCKS_PER_STEP = 2
VMEM_LIMIT = 56 * 1024 * 1024


def _params(*sem):
    return pltpu.CompilerParams(dimension_semantics=sem, vmem_limit_bytes=VMEM_LIMIT)


def _sigmoid(x):
    return 1.0 / (1.0 + jnp.exp(-x))


def _silu(x):
    return x * _sigmoid(x)


def _rmsnorm_kernel(x_ref, g_ref, o_ref):
    x = x_ref[...]
    ms = jnp.mean(x * x, axis=-1, keepdims=True)
    o_ref[...] = (x * lax.rsqrt(ms + EPS) * g_ref[...]).astype(o_ref.dtype)


def _rmsnorm(x, g, out_dtype):
    t, d = x.shape
    return pl.pallas_call(
        _rmsnorm_kernel,
        out_shape=jax.ShapeDtypeStruct((t, d), out_dtype),
        grid=(t // ROW_TILE,),
        in_specs=[pl.BlockSpec((ROW_TILE, d), lambda i: (i, 0)),
                  pl.BlockSpec((1, d), lambda i: (0, 0))],
        out_specs=pl.BlockSpec((ROW_TILE, d), lambda i: (i, 0)),
        compiler_params=_params("parallel"),
        name="rmsnorm",
    )(x, g.reshape(1, d))


def _matmul_kernel(a_ref, b_ref, o_ref):
    o_ref[...] = jnp.dot(a_ref[...], b_ref[...], preferred_element_type=jnp.float32)


def _matmul(a, b, tm, tn):
    t, k = a.shape
    n = b.shape[1]
    return pl.pallas_call(
        _matmul_kernel,
        out_shape=jax.ShapeDtypeStruct((t, n), jnp.float32),
        grid=(t // tm, n // tn),
        in_specs=[pl.BlockSpec((tm, k), lambda i, j: (i, 0)),
                  pl.BlockSpec((k, tn), lambda i, j: (0, j))],
        out_specs=pl.BlockSpec((tm, tn), lambda i, j: (i, j)),
        compiler_params=_params("parallel", "arbitrary"),
        name="in_proj",
    )(a, b)


def _pool_prompt_kernel(u_ref, halo_ref, z_ref, w_ref, sc_ref, o_ref):
    not_first = jnp.minimum(pl.program_id(0), 1) * ROW_TILE
    r = lax.broadcasted_iota(jnp.int32, (ROW_TILE, ROW_TILE), 0)
    c = lax.broadcasted_iota(jnp.int32, (ROW_TILE, ROW_TILE), 1)
    rh = lax.broadcasted_iota(jnp.int32, (ROW_TILE, HALO), 0)
    ch = lax.broadcasted_iota(jnp.int32, (ROW_TILE, HALO), 1)
    rcol = lax.broadcasted_iota(jnp.int32, (ROW_TILE, 1), 0)
    for g, w in enumerate(POOL_WINDOWS):
        sl = slice(g * POOL_GROUP, (g + 1) * POOL_GROUP)
        band = ((c <= r) & (c > r - w)).astype(jnp.float32).astype(jnp.bfloat16)
        band_halo = (ch + not_first > rh - w + HALO + ROW_TILE).astype(jnp.float32).astype(jnp.bfloat16)
        cnt = jnp.minimum(rcol + 1 + not_first, w).astype(jnp.float32)
        u = u_ref[:, sl]
        wsum = (jnp.dot(band, u.astype(jnp.bfloat16), preferred_element_type=jnp.float32)
                + jnp.dot(band_halo, halo_ref[:, sl].astype(jnp.bfloat16), preferred_element_type=jnp.float32))
        d = wsum / cnt - u
        y = jnp.dot(d.astype(jnp.bfloat16), w_ref[g], preferred_element_type=jnp.float32)
        o_ref[:, sl] = (y * sc_ref[:, sl] * _silu(z_ref[:, sl])).astype(o_ref.dtype)


def _pool_prompt(proj, w_pool, pool_scale):
    t = proj.shape[0]
    halo_blocks = ROW_TILE // HALO
    return pl.pallas_call(
        _pool_prompt_kernel,
        out_shape=jax.ShapeDtypeStruct((t, W_POOL), jnp.bfloat16),
        grid=(t // ROW_TILE,),
        in_specs=[pl.BlockSpec((ROW_TILE, W_POOL), lambda i: (i, OFF_U // W_POOL)),
                  pl.BlockSpec((HALO, W_POOL), lambda i: (jnp.maximum(i * halo_blocks - 1, 0), OFF_U // W_POOL)),
                  pl.BlockSpec((ROW_TILE, W_POOL), lambda i: (i, OFF_ZP // W_POOL)),
                  pl.BlockSpec((N_POOL_GROUPS, POOL_GROUP, POOL_GROUP), lambda i: (0, 0, 0)),
                  pl.BlockSpec((1, W_POOL), lambda i: (0, 0))],
        out_specs=pl.BlockSpec((ROW_TILE, W_POOL), lambda i: (i, 0)),
        compiler_params=_params("parallel"),
        name="pool_prompt",
    )(proj, proj, proj, w_pool, pool_scale)


def _pool_sample_kernel(u_ref, st_ref, z_ref, w_ref, sc_ref, o_ref, *, dec_seq):
    n_rows = u_ref.shape[0]
    n_state = st_ref.shape[0]
    w = jnp.left_shift(2, pl.program_id(0))
    seq_shift = dec_seq.bit_length() - 1
    r = lax.broadcasted_iota(jnp.int32, (n_rows, n_rows), 0)
    c = lax.broadcasted_iota(jnp.int32, (n_rows, n_rows), 1)
    same_seq = (r >> seq_shift) == (c >> seq_shift)
    band = (same_seq & (c <= r) & (r - c < w)).astype(jnp.float32).astype(jnp.bfloat16)
    rs = lax.broadcasted_iota(jnp.int32, (n_rows, n_state), 0)
    cs = lax.broadcasted_iota(jnp.int32, (n_rows, n_state), 1)
    base = (rs >> seq_shift) * POOL_BUF
    tok = rs & (dec_seq - 1)
    band_state = ((cs >= base + POOL_BUF + 1 + tok - w) & (cs < base + POOL_BUF)).astype(jnp.float32).astype(jnp.bfloat16)
    u = u_ref[...]
    wsum = (jnp.dot(band, u.astype(jnp.bfloat16), preferred_element_type=jnp.float32)
            + jnp.dot(band_state, st_ref[...].astype(jnp.bfloat16), preferred_element_type=jnp.float32))
    cnt = (jnp.zeros((n_rows, 1), jnp.int32) + w).astype(jnp.float32)
    d = wsum / cnt - u
    y = jnp.dot(d.astype(jnp.bfloat16), w_ref[0], preferred_element_type=jnp.float32)
    o_ref[...] = (y * sc_ref[...] * _silu(z_ref[...])).astype(o_ref.dtype)


def _pool_sample(proj, state2d, w_pool, pool_scale, dec_seq):
    t = proj.shape[0]
    n_state = state2d.shape[0]
    return pl.pallas_call(
        functools.partial(_pool_sample_kernel, dec_seq=dec_seq),
        out_shape=jax.ShapeDtypeStruct((t, W_POOL), jnp.bfloat16),
        grid=(N_POOL_GROUPS,),
        in_specs=[pl.BlockSpec((t, POOL_GROUP), lambda g: (0, OFF_U // POOL_GROUP + g)),
                  pl.BlockSpec((n_state, POOL_GROUP), lambda g: (0, g)),
                  pl.BlockSpec((t, POOL_GROUP), lambda g: (0, OFF_ZP // POOL_GROUP + g)),
                  pl.BlockSpec((1, POOL_GROUP, POOL_GROUP), lambda g: (g, 0, 0)),
                  pl.BlockSpec((1, POOL_GROUP), lambda g: (0, g))],
        out_specs=pl.BlockSpec((t, POOL_GROUP), lambda g: (0, g)),
        compiler_params=_params("parallel"),
        name="pool_sample",
    )(proj, state2d, proj, w_pool, pool_scale)


def _rotary(x, cos, sa, sb):
    return x * cos + pltpu.roll(x, HEAD_DIM - ROT_DIM // 2, 1) * sa + pltpu.roll(x, ROT_DIM // 2, 1) * sb


def _rope_prompt_kernel(q_ref, k_ref, v_ref, c_ref, sa_ref, sb_ref,
                        qt_ref, ko_ref, kb_ref, vo_ref, vt_ref, km_ref):
    cos, sa, sb = c_ref[...], sa_ref[...], sb_ref[...]
    for hd in range(N_HEADS):
        sl = slice(hd * HEAD_DIM, (hd + 1) * HEAD_DIM)
        qt_ref[sl, :] = (_rotary(q_ref[:, sl], cos, sa, sb) * Q_SCALE).T.astype(qt_ref.dtype)
    for hd in range(N_KV_HEADS):
        sl = slice(hd * HEAD_DIM, (hd + 1) * HEAD_DIM)
        kr = _rotary(k_ref[:, sl], cos, sa, sb)
        ko_ref[:, sl] = kr
        kb_ref[:, sl] = kr.astype(kb_ref.dtype)
        km_ref[0, :, sl] = jnp.mean(kr, axis=0, keepdims=True)
        v = v_ref[:, sl]
        vo_ref[:, sl] = v
        vt_ref[0, sl, :] = v.T.astype(vt_ref.dtype)


def _rope_prompt(proj, cos, sa, sb):
    t = proj.shape[0]
    assert ROW_TILE == MOBA_BLOCK
    nt = t // ROW_TILE
    row = lambda i: (i, 0)
    return pl.pallas_call(
        _rope_prompt_kernel,
        out_shape=[jax.ShapeDtypeStruct((W_ATTN, t), jnp.bfloat16),
                   jax.ShapeDtypeStruct((t, W_KV), jnp.float32),
                   jax.ShapeDtypeStruct((t, W_KV), jnp.bfloat16),
                   jax.ShapeDtypeStruct((t, W_KV), jnp.float32),
                   jax.ShapeDtypeStruct((nt, W_KV, ROW_TILE), jnp.bfloat16),
                   jax.ShapeDtypeStruct((nt, 1, W_KV), jnp.float32)],
        grid=(nt,),
        in_specs=[pl.BlockSpec((ROW_TILE, W_ATTN), lambda i: (i, OFF_Q // W_ATTN)),
                  pl.BlockSpec((ROW_TILE, W_KV), lambda i: (i, OFF_K // W_KV)),
                  pl.BlockSpec((ROW_TILE, W_KV), lambda i: (i, OFF_V // W_KV)),
                  pl.BlockSpec((ROW_TILE, HEAD_DIM), row),
                  pl.BlockSpec((ROW_TILE, HEAD_DIM), row),
                  pl.BlockSpec((ROW_TILE, HEAD_DIM), row)],
        out_specs=[pl.BlockSpec((W_ATTN, ROW_TILE), lambda i: (0, i)),
                   pl.BlockSpec((ROW_TILE, W_KV), row),
                   pl.BlockSpec((ROW_TILE, W_KV), row),
                   pl.BlockSpec((ROW_TILE, W_KV), row),
                   pl.BlockSpec((1, W_KV, ROW_TILE), lambda i: (i, 0, 0)),
                   pl.BlockSpec((1, 1, W_KV), lambda i: (i, 0, 0))],
        compiler_params=_params("parallel"),
        name="rope_prompt",
    )(proj, proj, proj, cos, sa, sb)


def _rope_sample_kernel(q_ref, k_ref, v_ref, c_ref, sa_ref, sb_ref, qo_ref, ko_ref, vo_ref):
    cos, sa, sb = c_ref[...], sa_ref[...], sb_ref[...]
    for hd in range(N_HEADS):
        sl = slice(hd * HEAD_DIM, (hd + 1) * HEAD_DIM)
        qo_ref[:, sl] = (_rotary(q_ref[:, sl], cos, sa, sb) * Q_SCALE).astype(qo_ref.dtype)
    for hd in range(N_KV_HEADS):
        sl = slice(hd * HEAD_DIM, (hd + 1) * HEAD_DIM)
        ko_ref[:, sl] = _rotary(k_ref[:, sl], cos, sa, sb)
    vo_ref[...] = v_ref[...]


def _rope_sample(proj, cos, sa, sb):
    t = proj.shape[0]
    row = lambda i: (i, 0)
    return pl.pallas_call(
        _rope_sample_kernel,
        out_shape=[jax.ShapeDtypeStruct((t, W_ATTN), jnp.bfloat16),
                   jax.ShapeDtypeStruct((t, W_KV), jnp.float32),
                   jax.ShapeDtypeStruct((t, W_KV), jnp.float32)],
        grid=(t // ROW_TILE,),
        in_specs=[pl.BlockSpec((ROW_TILE, W_ATTN), lambda i: (i, OFF_Q // W_ATTN)),
                  pl.BlockSpec((ROW_TILE, W_KV), lambda i: (i, OFF_K // W_KV)),
                  pl.BlockSpec((ROW_TILE, W_KV), lambda i: (i, OFF_V // W_KV)),
                  pl.BlockSpec((ROW_TILE, HEAD_DIM), row),
                  pl.BlockSpec((ROW_TILE, HEAD_DIM), row),
                  pl.BlockSpec((ROW_TILE, HEAD_DIM), row)],
        out_specs=[pl.BlockSpec((ROW_TILE, W_ATTN), row),
                   pl.BlockSpec((ROW_TILE, W_KV), row),
                   pl.BlockSpec((ROW_TILE, W_KV), row)],
        compiler_params=_params("parallel"),
        name="rope_sample",
    )(proj, proj, proj, cos, sa, sb)


def _rope_tables(pos):
    inv = ROPE_THETA ** (-jnp.arange(0, ROT_DIM, 2, dtype=jnp.float32) / ROT_DIM)
    ang = pos.astype(jnp.float32)[:, None] * inv[None, :]
    c, s = jnp.cos(ang), jnp.sin(ang)
    half = ROT_DIM // 2
    n = pos.shape[0]
    pad = HEAD_DIM - ROT_DIM
    cos = jnp.concatenate([c, c, jnp.ones((n, pad), jnp.float32)], axis=1)
    sa = jnp.concatenate([-s, jnp.zeros((n, half + pad), jnp.float32)], axis=1)
    sb = jnp.concatenate([jnp.zeros((n, half), jnp.float32), s, jnp.zeros((n, pad), jnp.float32)], axis=1)
    return cos, sa, sb


def _select_topk(gate, n_past):
    nrow = gate.shape[0]
    row = lax.broadcasted_iota(jnp.int32, gate.shape, 0)
    rowf = row.astype(jnp.float32)
    past = row < n_past
    cur = jnp.where(past, gate, NEG)
    bias = jnp.full(gate.shape, NEG, jnp.float32)
    for _ in range(MOBA_TOPK):
        top = jnp.max(cur, axis=0, keepdims=True)
        idx = jnp.min(jnp.where(cur == top, rowf, float(nrow)), axis=0, keepdims=True)
        hit = rowf == idx
        bias = jnp.where(hit & past, 0.0, bias)
        cur = jnp.where(hit, -jnp.inf, cur)
    return bias


def _moba_prompt_kernel(qt_ref, k_ref, vt_ref, km_ref, z_ref, o_ref, bias_s):
    i = pl.program_id(1)
    km = km_ref[...].astype(jnp.bfloat16)
    kpos = lax.broadcasted_iota(jnp.int32, (MOBA_BLOCK, MOBA_BLOCK), 0)
    qpos = lax.broadcasted_iota(jnp.int32, (MOBA_BLOCK, MOBA_BLOCK), 1)
    causal = jnp.where(kpos <= qpos, 0.0, NEG)

    hsl = [slice(g * HEAD_DIM, (g + 1) * HEAD_DIM) for g in range(GQA)]
    qts = [qt_ref[sl, :] for sl in hsl]
    for g in range(GQA):
        bias_s[g] = _select_topk(jnp.dot(km, qts[g], preferred_element_type=jnp.float32), i)

    def attend(carry, blocks):
        vt = jnp.concatenate([vt_ref[j] for j, _ in blocks], axis=1)
        new = []
        for g in range(GQA):
            m, l, acc = carry[g]
            sts = [jnp.dot(k_ref[pl.ds(pl.multiple_of(j * MOBA_BLOCK, MOBA_BLOCK), MOBA_BLOCK), :], qts[g],
                           preferred_element_type=jnp.float32) + bias[g] for j, bias in blocks]
            m_new = functools.reduce(jnp.maximum, [m] + [jnp.max(st, axis=0, keepdims=True) for st in sts])
            alpha = jnp.exp2(m - m_new)
            pts = [jnp.exp2(st - m_new) for st in sts]
            l = alpha * l + sum(jnp.sum(pt, axis=0, keepdims=True) for pt in pts)
            pt = jnp.concatenate([pt.astype(jnp.bfloat16) for pt in pts], axis=0)
            acc = alpha * acc + jnp.dot(vt, pt, preferred_element_type=jnp.float32)
            new.append((m_new, l, acc))
        return tuple(new)

    def selected(j):
        return [bias_s[g, pl.ds(j, 1), :] for g in range(GQA)]

    def past_pair(k, carry):
        return attend(carry, [(2 * k, selected(2 * k)), (2 * k + 1, selected(2 * k + 1))])

    init = (jnp.full((1, MOBA_BLOCK), NEG, jnp.float32), jnp.zeros((1, MOBA_BLOCK), jnp.float32),
            jnp.zeros((HEAD_DIM, MOBA_BLOCK), jnp.float32))
    carry = lax.fori_loop(0, lax.shift_right_logical(i, 1), past_pair, (init,) * GQA)
    last = jnp.maximum(i - 1, 0)
    unpaired = jnp.where((jnp.zeros((1, MOBA_BLOCK), jnp.int32) + (i & 1)) == 1, 0.0, NEG)
    carry = attend(carry, [(last, [b + unpaired for b in selected(last)]), (i, [causal] * GQA)])
    for g in range(GQA):
        _, l, acc = carry[g]
        o_ref[:, hsl[g]] = ((acc / l).T * _silu(z_ref[:, hsl[g]])).astype(o_ref.dtype)


def _moba_prompt(qt, k, vt, kmean, proj):
    t = k.shape[0]
    nb = t // MOBA_BLOCK
    gw = GQA * HEAD_DIM
    return pl.pallas_call(
        _moba_prompt_kernel,
        out_shape=jax.ShapeDtypeStruct((t, W_ATTN), jnp.bfloat16),
        grid=(N_KV_HEADS, nb),
        in_specs=[pl.BlockSpec((gw, MOBA_BLOCK), lambda h, i: (h, i)),
                  pl.BlockSpec((t, HEAD_DIM), lambda h, i: (0, h)),
                  pl.BlockSpec((nb, HEAD_DIM, MOBA_BLOCK), lambda h, i: (0, h, 0)),
                  pl.BlockSpec((nb, HEAD_DIM), lambda h, i: (0, h)),
                  pl.BlockSpec((MOBA_BLOCK, gw), lambda h, i: (i, OFF_ZA // gw + h))],
        out_specs=pl.BlockSpec((MOBA_BLOCK, gw), lambda h, i: (i, h)),
        scratch_shapes=[pltpu.VMEM((GQA, nb, MOBA_BLOCK), jnp.float32)],
        compiler_params=_params("parallel", "arbitrary"),
        name="moba_prompt",
    )(qt, k, vt, kmean, proj)


def _moba_sample_kernel(pt_ref, q_ref, *refs, dec_seq, blocks_per_step):
    n_pages = blocks_per_step * PAGES_PER_BLOCK
    k_refs, v_refs = refs[:n_pages], refs[n_pages:2 * n_pages]
    kn_ref, vn_ref, z_ref, o_ref, bias_s, g_s, m_s, l_s, o_s = refs[2 * n_pages:]
    step = pl.program_id(1)
    rows_per_head = GQA * dec_seq
    rows = N_KV_HEADS * rows_per_head
    page_rows = PAGE_SIZE * N_KV_HEADS
    full = (rows, HEAD_DIM)
    contract_last = (((1,), (1,)), ((), ()))

    @pl.when(step == 0)
    def _():
        row_head = lax.broadcasted_iota(jnp.int32, (rows, page_rows), 0) >> (rows_per_head.bit_length() - 1)
        col_head = lax.broadcasted_iota(jnp.int32, (rows, page_rows), 1) & (N_KV_HEADS - 1)
        bias_s[...] = jnp.where(row_head == col_head, 0.0, NEG)

    def head_rows(piece):
        return jnp.concatenate([jnp.broadcast_to(piece(h), (rows_per_head, HEAD_DIM)) for h in range(N_KV_HEADS)],
                               axis=0)

    q = q_ref[0]
    qf = q.astype(jnp.float32)
    for blk in range(blocks_per_step):
        kp = [k_refs[blk * PAGES_PER_BLOCK + p][0] for p in range(PAGES_PER_BLOCK)]
        vp = [v_refs[blk * PAGES_PER_BLOCK + p][0] for p in range(PAGES_PER_BLOCK)]
        kmean = sum(x.reshape(PAGE_SIZE, N_KV_HEADS, HEAD_DIM).sum(axis=0) for x in kp) * (1.0 / MOBA_BLOCK)
        gate = jnp.sum(qf * head_rows(lambda h: kmean[h:h + 1, :]), axis=1, keepdims=True)
        s = [lax.dot_general(q, x.astype(jnp.bfloat16), contract_last, preferred_element_type=jnp.float32)
             + bias_s[...] for x in kp]
        m = functools.reduce(jnp.maximum, [jnp.max(x, axis=1, keepdims=True) for x in s])
        p = [jnp.exp2(x - m) for x in s]
        l = sum(jnp.sum(x, axis=1, keepdims=True) for x in p)
        o = sum(jnp.dot(x.astype(jnp.bfloat16), y.astype(jnp.bfloat16), preferred_element_type=jnp.float32)
                for x, y in zip(p, vp))
        b = step * blocks_per_step + blk
        g_s[b] = jnp.broadcast_to(gate, full)
        m_s[b] = jnp.broadcast_to(m, full)
        l_s[b] = jnp.broadcast_to(l, full)
        o_s[b] = o

    @pl.when(step == pl.num_programs(1) - 1)
    def _():
        nb = g_s.shape[0]
        gates = [g_s[b] for b in range(nb)]
        sel = [jnp.zeros(full, jnp.bool_) for _ in range(nb)]
        for _ in range(min(MOBA_TOPK, nb)):
            top = functools.reduce(jnp.maximum, gates)
            idx = functools.reduce(jnp.minimum,
                                   [jnp.where(gates[b] == top, float(b), float(nb)) for b in range(nb)])
            for b in range(nb):
                hit = idx == float(b)
                sel[b] = sel[b] | hit
                gates[b] = jnp.where(hit, -jnp.inf, gates[b])
        tok = (lax.broadcasted_iota(jnp.int32, full, 0) >> (GQA.bit_length() - 1)) & (dec_seq - 1)
        hsl = lambda h: slice(h * HEAD_DIM, (h + 1) * HEAD_DIM)
        s_new = [jnp.broadcast_to(
            jnp.sum(qf * head_rows(lambda h: kn_ref[0, r:r + 1, hsl(h)]), axis=1, keepdims=True), full)
            for r in range(dec_seq)]
        ok_new = [tok >= r for r in range(dec_seq)]
        ms = [m_s[b] for b in range(nb)]
        top = functools.reduce(jnp.maximum,
                               [jnp.where(sel[b], ms[b], NEG) for b in range(nb)]
                               + [jnp.where(ok_new[r], s_new[r], NEG) for r in range(dec_seq)])
        num = jnp.zeros(full, jnp.float32)
        den = jnp.zeros(full, jnp.float32)
        for b in range(nb):
            wgt = jnp.where(sel[b], jnp.exp2(ms[b] - top), 0.0)
            num = num + wgt * o_s[b]
            den = den + wgt * l_s[b]
        for r in range(dec_seq):
            wgt = jnp.where(ok_new[r], jnp.exp2(s_new[r] - top), 0.0)
            num = num + wgt * head_rows(lambda h: vn_ref[0, r:r + 1, hsl(h)])
            den = den + wgt
        o_ref[0] = (num / den * _silu(z_ref[0])).astype(o_ref.dtype)


def _moba_sample(page_table, q3, cache_k, cache_v, k_new, v_new, z3, dec_seq):
    nseq, n_pages = page_table.shape
    n_blocks = n_pages // PAGES_PER_BLOCK
    blocks_per_step = SAMPLE_BLOCKS_PER_STEP
    assert n_blocks % blocks_per_step == 0
    pages_per_step = blocks_per_step * PAGES_PER_BLOCK
    rows = N_KV_HEADS * GQA * dec_seq
    page_rows = PAGE_SIZE * N_KV_HEADS
    seq3 = pl.BlockSpec((1, rows, HEAD_DIM), lambda b, j, pt: (b, 0, 0))
    pages = [pl.BlockSpec((1, page_rows, HEAD_DIM),
                          functools.partial(lambda off, b, j, pt: (pt[b, pages_per_step * j + off], 0, 0), off))
             for off in range(pages_per_step)]
    new = pl.BlockSpec((1, dec_seq, W_KV), lambda b, j, pt: (b, 0, 0))
    stat = pltpu.VMEM((n_blocks, rows, HEAD_DIM), jnp.float32)
    return pl.pallas_call(
        functools.partial(_moba_sample_kernel, dec_seq=dec_seq, blocks_per_step=blocks_per_step),
        out_shape=jax.ShapeDtypeStruct((nseq, rows, HEAD_DIM), jnp.bfloat16),
        grid_spec=pltpu.PrefetchScalarGridSpec(
            num_scalar_prefetch=1,
            grid=(nseq, n_blocks // blocks_per_step),
            in_specs=[seq3] + pages + pages + [new, new, seq3],
            out_specs=seq3,
            scratch_shapes=[pltpu.VMEM((rows, page_rows), jnp.float32), stat, stat, stat, stat]),
        compiler_params=_params("parallel", "arbitrary"),
        name="moba_sample",
    )(page_table, q3, *([cache_k] * pages_per_step), *([cache_v] * pages_per_step), k_new, v_new, z3)


def _merge_kernel(ap_ref, aa_ref, wp_ref, wa_ref, gp_ref, ga_ref, o_ref):
    bp = jnp.dot(ap_ref[...], wp_ref[...], preferred_element_type=jnp.float32)
    ba = jnp.dot(aa_ref[...], wa_ref[...], preferred_element_type=jnp.float32)
    o_ref[...] = (_sigmoid(gp_ref[...]) * bp + _sigmoid(ga_ref[...]) * ba).astype(o_ref.dtype)


def _merge(a_pool, a_attn, w_bp, w_ba, proj, tm, tn):
    t = a_pool.shape[0]
    return pl.pallas_call(
        _merge_kernel,
        out_shape=jax.ShapeDtypeStruct((t, D_MODEL), jnp.bfloat16),
        grid=(t // tm, D_MODEL // tn),
        in_specs=[pl.BlockSpec((tm, W_POOL), lambda i, j: (i, 0)),
                  pl.BlockSpec((tm, W_ATTN), lambda i, j: (i, 0)),
                  pl.BlockSpec((W_POOL, tn), lambda i, j: (0, j)),
                  pl.BlockSpec((W_ATTN, tn), lambda i, j: (0, j)),
                  pl.BlockSpec((tm, tn), lambda i, j: (i, OFF_GP // tn + j)),
                  pl.BlockSpec((tm, tn), lambda i, j: (i, OFF_GA // tn + j))],
        out_specs=pl.BlockSpec((tm, tn), lambda i, j: (i, j)),
        compiler_params=_params("parallel", "arbitrary"),
        name="merge",
    )(a_pool, a_attn, w_bp, w_ba, proj, proj)


def _out_proj_kernel(m_ref, w_ref, x_ref, o_ref):
    o_ref[...] = x_ref[...] + jnp.dot(m_ref[...], w_ref[...], preferred_element_type=jnp.float32)


def _out_proj(merged, w_o, x, tm, tn):
    t = merged.shape[0]
    return pl.pallas_call(
        _out_proj_kernel,
        out_shape=jax.ShapeDtypeStruct((t, D_MODEL), jnp.float32),
        grid=(t // tm, D_MODEL // tn),
        in_specs=[pl.BlockSpec((tm, D_MODEL), lambda i, j: (i, 0)),
                  pl.BlockSpec((D_MODEL, tn), lambda i, j: (0, j)),
                  pl.BlockSpec((tm, tn), lambda i, j: (i, j))],
        out_specs=pl.BlockSpec((tm, tn), lambda i, j: (i, j)),
        compiler_params=_params("parallel", "arbitrary"),
        name="out_proj",
    )(merged, w_o, x)


def _to_kv_rows(x, nseq, dec_seq):
    x = x.reshape(nseq, dec_seq, N_KV_HEADS, GQA, HEAD_DIM)
    return x.transpose(0, 2, 1, 3, 4).reshape(nseq, N_KV_HEADS * dec_seq * GQA, HEAD_DIM)


def _from_kv_rows(x, nseq, dec_seq):
    x = x.reshape(nseq, N_KV_HEADS, dec_seq, GQA, HEAD_DIM)
    return x.transpose(0, 2, 1, 3, 4).reshape(nseq * dec_seq, W_ATTN)


def kernel(x_prompt, x_sample, cache_k, cache_v, state_pool, page_table, norm_in, w_in, w_pool, pool_scale,
           w_branch_pool, w_branch_attn, w_out, norm_final):
    batch, seq, _ = x_prompt.shape
    nseq, dec_seq, _ = x_sample.shape
    depth, n_phys = cache_k.shape[:2]
    assert batch == 1 and depth == 1
    past = page_table.shape[1] * PAGE_SIZE
    assert past % MOBA_BLOCK == 0 and dec_seq <= MOBA_BLOCK
    assert dec_seq & (dec_seq - 1) == 0 and GQA & (GQA - 1) == 0
    bf16 = jnp.bfloat16

    w_in_b = w_in[0].astype(bf16)
    w_pool_b = w_pool[0].astype(bf16)
    w_bp_b = w_branch_pool[0].astype(bf16)
    w_ba_b = w_branch_attn[0].astype(bf16)
    w_o_b = w_out[0].astype(bf16)
    scale2d = pool_scale[0].reshape(1, W_POOL)

    def trunk(x2d, pos, tm, pool_fn, attend_fn):
        h = _rmsnorm(x2d, norm_in[0], bf16)
        proj = _matmul(h, w_in_b, tm, 1024)
        a_pool = pool_fn(proj)
        a_attn, k, v = attend_fn(proj, *_rope_tables(pos))
        merged = _merge(a_pool, a_attn, w_bp_b, w_ba_b, proj, 512, 1024)
        y = _out_proj(merged, w_o_b, x2d, 512, 1024)
        return _rmsnorm(y, norm_final, jnp.float32), k, v, proj

    def _attend_prompt(proj, cos, sa, sb):
        qt, k, kb, v, vt, km = _rope_prompt(proj, cos, sa, sb)
        return _moba_prompt(qt, kb, vt, km.reshape(km.shape[0], W_KV), proj), k, v

    def _attend_sample(proj, cos, sa, sb):
        q, k, v = _rope_sample(proj, cos, sa, sb)
        q3 = _to_kv_rows(q, nseq, dec_seq)
        z3 = _to_kv_rows(proj[:, OFF_ZA:OFF_ZA + W_ATTN], nseq, dec_seq)
        ck = cache_k[0].reshape(n_phys, PAGE_SIZE * N_KV_HEADS, HEAD_DIM)
        cv = cache_v[0].reshape(n_phys, PAGE_SIZE * N_KV_HEADS, HEAD_DIM)
        a3 = _moba_sample(page_table, q3, ck, cv, k.reshape(nseq, dec_seq, W_KV),
                          v.reshape(nseq, dec_seq, W_KV), z3, dec_seq)
        return _from_kv_rows(a3, nseq, dec_seq), k, v

    pos_p = jnp.arange(seq, dtype=jnp.int32)
    pos_s = jnp.tile(past + jnp.arange(dec_seq, dtype=jnp.int32), nseq)
    state2d = state_pool[0].reshape(nseq * POOL_BUF, W_POOL)

    y_p, k_p, v_p, proj_p = trunk(
        x_prompt.reshape(seq, D_MODEL), pos_p, 1024,
        lambda proj: _pool_prompt(proj, w_pool_b, scale2d), _attend_prompt)
    y_s, k_s, v_s, proj_s = trunk(
        x_sample.reshape(nseq * dec_seq, D_MODEL), pos_s, nseq * dec_seq,
        lambda proj: _pool_sample(proj, state2d, w_pool_b, scale2d, dec_seq), _attend_sample)

    u_s = proj_s[:, OFF_U:OFF_U + W_POOL].reshape(nseq, dec_seq, W_POOL)
    pool_s = jnp.concatenate([state_pool[0], u_s], axis=1)[:, -POOL_BUF:]
    pool_p = proj_p[seq - POOL_BUF:, OFF_U:OFF_U + W_POOL]
    return (y_p.reshape(1, seq, D_MODEL),
            y_s.reshape(nseq, dec_seq, D_MODEL),
            k_p.reshape(1, 1, seq, N_KV_HEADS, HEAD_DIM),
            v_p.reshape(1, 1, seq, N_KV_HEADS, HEAD_DIM),
            pool_p.reshape(1, 1, POOL_BUF, W_POOL),
            k_s.reshape(1, nseq, dec_seq, N_KV_HEADS, HEAD_DIM),
            v_s.reshape(1, nseq, dec_seq, N_KV_HEADS, HEAD_DIM),
            pool_s.reshape(1, nseq, POOL_BUF, W_POOL))
```
